```python
import math
import jax, jax.numpy as jnp
from jax import lax
import numpy as np

D_MODEL = 1024
BATCH = 32
SEQ = 2048
DEPTH = 1

RET_HEADS = 4
RET_DV = D_MODEL // RET_HEADS
RET_DK = RET_DV // 2
RET_QK = RET_HEADS * RET_DK
RET_V = RET_HEADS * RET_DV
RET_CHUNK = 128
ROPE_BASE = 10000.0
CONV_WIDTH = D_MODEL
CONV_KERNEL = 31
PEER_HEADS = 8
PEER_DKEY = 256
PEER_NKEYS = 128
PEER_EXPERTS = PEER_NKEYS * PEER_NKEYS
PEER_TOPK = 16
PEER_BLOCK = 128
EPS = 1e-6
N_ADA = 6

kernel_name = "hybrid_retention_conformer_peer_block"


def _in_splits():
    widths = [RET_QK, RET_QK, RET_V, RET_V, CONV_WIDTH, CONV_WIDTH, D_MODEL, D_MODEL]
    return [int(s) for s in np.cumsum(widths)[:-1]]


def rms_norm(x, gain):
    xf = x.astype(jnp.float32)
    y = xf * lax.rsqrt(jnp.mean(xf * xf, axis=-1, keepdims=True) + EPS)
    return (y * gain.astype(jnp.float32)).astype(x.dtype)


def layer_norm(x, gain, bias):
    xf = x.astype(jnp.float32)
    mu = jnp.mean(xf, axis=-1, keepdims=True)
    var = jnp.mean(jnp.square(xf - mu), axis=-1, keepdims=True)
    y = (xf - mu) * lax.rsqrt(var + EPS)
    return (y * gain.astype(jnp.float32) + bias.astype(jnp.float32)).astype(x.dtype)


def rotary(t, positions):
    half = t.shape[-1] // 2
    inv_freq = ROPE_BASE ** (-jnp.arange(half, dtype=jnp.float32) / half)
    ang = positions.astype(jnp.float32)[:, :, None] * inv_freq
    cos = jnp.cos(ang)[:, :, None, :]
    sin = jnp.sin(ang)[:, :, None, :]
    t1 = t[..., :half].astype(jnp.float32)
    t2 = t[..., half:].astype(jnp.float32)
    out = jnp.concatenate([t1 * cos - t2 * sin, t1 * sin + t2 * cos], axis=-1)
    return out.astype(t.dtype)


def retention_chunkwise(q, k, v):
    B, S, H, dk = q.shape
    dv = v.shape[-1]
    C = RET_CHUNK
    n = S // C
    log_gamma = jnp.log1p(-jnp.exp2(-5.0 - jnp.arange(H, dtype=jnp.float32)))
    idx = jnp.arange(C, dtype=jnp.float32)
    diff = idx[:, None] - idx[None, :]
    decay_mask = jnp.where(diff[None] >= 0,
                           jnp.exp(jnp.maximum(diff, 0.0)[None] * log_gamma[:, None, None]),
                           0.0)
    zeta = jnp.exp((C - 1 - idx)[None, :] * log_gamma[:, None])
    xi = jnp.exp((idx + 1)[None, :] * log_gamma[:, None])
    chunk_decay = jnp.exp(C * log_gamma)

    def to_chunks(t):
        return t.reshape(B, n, C, H, t.shape[-1]).transpose(1, 0, 3, 2, 4)

    def step(state, chunk):
        qc, kc, vc = chunk
        scores = jnp.einsum('bhid,bhjd->bhij', qc, kc) * decay_mask[None]
        inner = jnp.einsum('bhij,bhje->bhie', scores, vc)
        cross = jnp.einsum('bhid,bhde->bhie', qc, state) * xi[None, :, :, None]
        new_state = state * chunk_decay[None, :, None, None] + jnp.einsum(
            'bhjd,bhje->bhde', kc * zeta[None, :, :, None], vc)
        return new_state, inner + cross

    state0 = jnp.zeros((B, H, dk, dv), jnp.float32)
    _, out = lax.scan(step, state0, (to_chunks(q), to_chunks(k), to_chunks(v)))
    return out.transpose(1, 0, 3, 2, 4).reshape(B, S, H, dv).astype(v.dtype)


def head_group_norm(y, gain):
    B, S, H, dv = y.shape
    yf = y.astype(jnp.float32)
    mu = jnp.mean(yf, axis=-1, keepdims=True)
    var = jnp.mean(jnp.square(yf - mu), axis=-1, keepdims=True)
    yn = ((yf - mu) * lax.rsqrt(var + EPS)).reshape(B, S, H * dv)
    return (yn * gain.astype(jnp.float32)).astype(y.dtype)


def hybrid_mixer(h, positions, w_in, ret_gn_gain, w_dw, b_dw, conv_ln_gain, conv_ln_bias,
                 w_pw2, w_out):
    B, S, _ = h.shape
    proj = h @ w_in
    q, k, v, g_ret, c_val, c_gate, gate_r, gate_c = jnp.split(proj, _in_splits(), axis=-1)
    q = rotary(q.reshape(B, S, RET_HEADS, RET_DK), positions)
    k = rotary(k.reshape(B, S, RET_HEADS, RET_DK), positions) * (RET_DK ** -0.5)
    v = v.reshape(B, S, RET_HEADS, RET_DV)
    ret = retention_chunkwise(q, k, v)
    ret = jax.nn.silu(g_ret) * head_group_norm(ret, ret_gn_gain)
    u = c_val * jax.nn.sigmoid(c_gate)
    u = lax.conv_general_dilated(u, w_dw, window_strides=(1,),
                                 padding=[(CONV_KERNEL - 1, 0)],
                                 dimension_numbers=('NWC', 'WIO', 'NWC'),
                                 feature_group_count=CONV_WIDTH) + b_dw
    conv = jax.nn.silu(layer_norm(u, conv_ln_gain, conv_ln_bias)) @ w_pw2
    merged = jax.nn.sigmoid(gate_r) * ret + jax.nn.sigmoid(gate_c) * conv
    return merged @ w_out


def peer_ffn(h, w_q, subkeys1, subkeys2, expert_u, expert_v):
    B, S, D = h.shape
    half = PEER_DKEY // 2
    blocks = h.reshape((B * S) // PEER_BLOCK, PEER_BLOCK, D)

    def block_fn(hb):
        qv = (hb @ w_q).reshape(PEER_BLOCK, PEER_HEADS, PEER_DKEY)
        s1 = jnp.einsum('thd,nd->thn', qv[..., :half], subkeys1)
        s2 = jnp.einsum('thd,nd->thn', qv[..., half:], subkeys2)
        v1, i1 = lax.top_k(s1, PEER_TOPK)
        v2, i2 = lax.top_k(s2, PEER_TOPK)
        cand_s = (v1[..., :, None] + v2[..., None, :]).reshape(PEER_BLOCK, PEER_HEADS, -1)
        cand_i = (i1[..., :, None] * PEER_NKEYS + i2[..., None, :]).reshape(PEER_BLOCK, PEER_HEADS, -1)
        top_s, pos = lax.top_k(cand_s, PEER_TOPK)
        eidx = jnp.take_along_axis(cand_i, pos, axis=-1)
        wts = jax.nn.softmax(top_s.astype(jnp.float32), axis=-1).astype(hb.dtype)
        ug = expert_u[eidx]
        act = jax.nn.gelu(jnp.einsum('thkd,td->thk', ug, hb))
        vg = expert_v[eidx]
        return jnp.einsum('thk,thkd->td', wts * act, vg)

    return lax.map(block_fn, blocks).reshape(B, S, D)


def setup_inputs(seed: int = 0) -> dict:
    key = jax.random.key(seed)
    ks = jax.random.split(key, 24)
    f32 = jnp.float32
    L = DEPTH
    n_in = 2 * RET_QK + 2 * RET_V + 2 * CONV_WIDTH + 2 * D_MODEL

    def nrm(k, shape, scale):
        return jax.random.normal(k, shape, f32) * scale

    def gain(k, shape):
        return 1.0 + 0.02 * jax.random.normal(k, shape, f32)

    x = jax.random.normal(ks[0], (BATCH, SEQ, D_MODEL), f32)
    c = jax.random.normal(ks[1], (BATCH, D_MODEL), f32)
    offsets = jax.random.randint(ks[2], (BATCH, 1), 0, 4096, dtype=jnp.int32)
    positions = (offsets + jnp.arange(SEQ, dtype=jnp.int32)[None, :]).astype(jnp.int32)
    return {
        "x": x,
        "c": c,
        "positions": positions,
        "w_ada": nrm(ks[3], (L, D_MODEL, N_ADA * D_MODEL), 0.5 * D_MODEL ** -0.5),
        "b_ada": nrm(ks[4], (L, N_ADA * D_MODEL), 0.01),
        "norm_mix_gain": gain(ks[5], (L, D_MODEL)),
        "w_in": nrm(ks[6], (L, D_MODEL, n_in), D_MODEL ** -0.5),
        "ret_gn_gain": gain(ks[7], (L, RET_V)),
        "w_dw": nrm(ks[8], (L, CONV_KERNEL, 1, CONV_WIDTH), CONV_KERNEL ** -0.5),
        "b_dw": nrm(ks[9], (L, CONV_WIDTH), 0.01),
        "conv_ln_gain": gain(ks[10], (L, CONV_WIDTH)),
        "conv_ln_bias": nrm(ks[11], (L, CONV_WIDTH), 0.01),
        "w_pw2": nrm(ks[12], (L, CONV_WIDTH, D_MODEL), CONV_WIDTH ** -0.5),
        "w_out": nrm(ks[13], (L, D_MODEL, D_MODEL), D_MODEL ** -0.5),
        "norm_ffn_gain": gain(ks[14], (L, D_MODEL)),
        "w_peer_q": nrm(ks[15], (L, D_MODEL, PEER_HEADS * PEER_DKEY), D_MODEL ** -0.5),
        "peer_subkeys1": nrm(ks[16], (L, PEER_NKEYS, PEER_DKEY // 2), (PEER_DKEY // 2) ** -0.5),
        "peer_subkeys2": nrm(ks[17], (L, PEER_NKEYS, PEER_DKEY // 2), (PEER_DKEY // 2) ** -0.5),
        "peer_u": nrm(ks[18], (L, PEER_EXPERTS, D_MODEL), D_MODEL ** -0.5),
        "peer_v": nrm(ks[19], (L, PEER_EXPERTS, D_MODEL), PEER_TOPK ** -0.5),
        "final_norm_gain": gain(ks[20], (D_MODEL,)),
    }


def reference(x, c, positions, w_ada, b_ada, norm_mix_gain, w_in, ret_gn_gain, w_dw, b_dw,
              conv_ln_gain, conv_ln_bias, w_pw2, w_out, norm_ffn_gain, w_peer_q,
              peer_subkeys1, peer_subkeys2, peer_u, peer_v, final_norm_gain):
    cond = jax.nn.silu(c)
    for l in range(DEPTH):
        ada = cond @ w_ada[l] + b_ada[l]
        sh1, sc1, g1, sh2, sc2, g2 = jnp.split(ada[:, None, :], N_ADA, axis=-1)
        h = rms_norm(x, norm_mix_gain[l]) * (1.0 + sc1) + sh1
        x = x + g1 * hybrid_mixer(h, positions, w_in[l], ret_gn_gain[l], w_dw[l], b_dw[l],
                                  conv_ln_gain[l], conv_ln_bias[l], w_pw2[l], w_out[l])
        h = rms_norm(x, norm_ffn_gain[l]) * (1.0 + sc2) + sh2
        x = x + g2 * peer_ffn(h, w_peer_q[l], peer_subkeys1[l], peer_subkeys2[l],
                              peer_u[l], peer_v[l])
    return rms_norm(x, final_norm_gain)
```

```python
import functools
import math

import jax
import jax.numpy as jnp
import numpy as np
from jax import lax
from jax.experimental import pallas as pl
from jax.experimental.pallas import tpu as pltpu

F32 = jnp.float32
BF16 = jnp.bfloat16

D_MODEL = 1024
RET_HEADS = 4
RET_DK = 128
RET_DV = 256
RET_QK = RET_HEADS * RET_DK
RET_V = RET_HEADS * RET_DV
RET_CHUNK = 128
ROPE_BASE = 10000.0
CONV_KERNEL = 31
CONV_HALO = 32
PEER_HEADS = 8
PEER_DKEY = 256
PEER_NKEYS = 128
PEER_TOPK = 16
EPS = 1e-6
N_ADA = 6

OFF_Q = 0
OFF_K = OFF_Q + RET_QK
OFF_V = OFF_K + RET_QK
OFF_G = OFF_V + RET_V
OFF_CV = OFF_G + RET_V
OFF_CG = OFF_CV + D_MODEL
OFF_GR = OFF_CG + D_MODEL
OFF_GC = OFF_GR + D_MODEL
N_IN = OFF_GC + D_MODEL

MIX_TILE = 256
PEER_TB = 512
PEER_EC = 512
VMEM_LIMIT = 56 * 1024 * 1024

GELU_C0 = math.sqrt(2.0 / math.pi)
GELU_C1 = 0.044715


def _sigmoid(v):
    return 1.0 / (1.0 + jnp.exp(-v))


def _const_spec(shape):
    nd = len(shape)
    return pl.BlockSpec(shape, lambda *_: (0,) * nd, pipeline_mode=pl.Buffered(1))


def _ada_kernel(c_ref, w_ref, b_ref, o_ref):
    cv = c_ref[...]
    cond = cv * _sigmoid(cv)
    o_ref[...] = jnp.dot(cond, w_ref[...], preferred_element_type=F32,
                         precision=lax.Precision.HIGHEST) + b_ref[...]


def _ada(c, w_ada, b_ada):
    B, D = c.shape
    N = w_ada.shape[1]
    tn = 1024
    return pl.pallas_call(
        _ada_kernel,
        grid=(N // tn,),
        in_specs=[pl.BlockSpec((B, D), lambda j: (0, 0)),
                  pl.BlockSpec((D, tn), lambda j: (0, j)),
                  pl.BlockSpec((1, tn), lambda j: (0, j))],
        out_specs=pl.BlockSpec((B, tn), lambda j: (0, j)),
        out_shape=jax.ShapeDtypeStruct((B, N), F32),
        compiler_params=pltpu.CompilerParams(vmem_limit_bytes=VMEM_LIMIT),
        name="ada",
    )(c, w_ada, b_ada.reshape(1, N))


def _mixer_kernel(x_ref, pos_ref, sh_ref, sc_ref, g_ref, ngain_ref, win_ref, invf_ref, sgn_ref,
                  dmask_ref, xi_ref, zeta_ref, gng_ref, wdw_ref, bdw_ref, lng_ref, lnb_ref,
                  wpw_ref, wout_ref, o_ref, state_ref, ubuf_ref, *, chunk_decay):
    T = MIX_TILE

    @pl.when(pl.program_id(1) == 0)
    def _():
        state_ref[...] = jnp.zeros_like(state_ref)
        ubuf_ref[0:CONV_HALO, :] = jnp.zeros((CONV_HALO, D_MODEL), F32)

    x = x_ref[0]
    ms = jnp.mean(x * x, axis=-1, keepdims=True)
    h = x * lax.rsqrt(ms + EPS) * ngain_ref[...]
    h = h * (1.0 + sc_ref[0]) + sh_ref[0]
    hb = h.astype(BF16)

    def proj(off, width):
        return jnp.dot(hb, win_ref[:, off:off + width], preferred_element_type=F32)

    ang = pos_ref[0] * invf_ref[...]
    cos = jnp.cos(ang)
    sin_signed = jnp.sin(ang) * sgn_ref[...]

    def rot(t):
        return t * cos + pltpu.roll(t, RET_DK // 2, 1) * sin_signed

    ret_heads = []
    for hd in range(RET_HEADS):
        qh = rot(proj(OFF_Q + hd * RET_DK, RET_DK))
        kh = rot(proj(OFF_K + hd * RET_DK, RET_DK)) * (RET_DK ** -0.5)
        vh = proj(OFF_V + hd * RET_DV, RET_DV)
        st = state_ref[hd]
        outs = []
        for ci in range(T // RET_CHUNK):
            rows = slice(ci * RET_CHUNK, (ci + 1) * RET_CHUNK)
            qc, kc, vc = qh[rows], kh[rows], vh[rows]
            qcb, kcb, vcb = qc.astype(BF16), kc.astype(BF16), vc.astype(BF16)
            scores = lax.dot_general(qcb, kcb, (((1,), (1,)), ((), ())),
                                     preferred_element_type=F32) * dmask_ref[hd]
            inner = jnp.dot(scores.astype(BF16), vcb, preferred_element_type=F32)
            cross = jnp.dot((qc * xi_ref[hd]).astype(BF16), st.astype(BF16),
                            preferred_element_type=F32)
            kz = (kc * zeta_ref[hd]).astype(BF16)
            st = st * chunk_decay[hd] + lax.dot_general(
                kz, vcb, (((0,), (0,)), ((), ())), preferred_element_type=F32)
            outs.append(inner + cross)
        state_ref[hd] = st
        y = jnp.concatenate(outs, axis=0)
        mu = jnp.mean(y, axis=-1, keepdims=True)
        yc = y - mu
        var = jnp.mean(yc * yc, axis=-1, keepdims=True)
        ret_heads.append(yc * lax.rsqrt(var + EPS))
    ret = jnp.concatenate(ret_heads, axis=1) * gng_ref[...]
    g_ret = proj(OFF_G, RET_V)
    ret = g_ret * _sigmoid(g_ret) * ret

    u = proj(OFF_CV, D_MODEL) * _sigmoid(proj(OFF_CG, D_MODEL))
    ubuf_ref[CONV_HALO:CONV_HALO + T, :] = u
    acc = jnp.broadcast_to(bdw_ref[...], (T, D_MODEL))
    base = CONV_HALO - (CONV_KERNEL - 1)
    for k in range(CONV_KERNEL):
        acc = acc + ubuf_ref[base + k:base + k + T, :] * wdw_ref[k:k + 1, :]
    ubuf_ref[0:CONV_HALO, :] = ubuf_ref[T:T + CONV_HALO, :]
    mu = jnp.mean(acc, axis=-1, keepdims=True)
    ac = acc - mu
    var = jnp.mean(ac * ac, axis=-1, keepdims=True)
    ln = ac * lax.rsqrt(var + EPS) * lng_ref[...] + lnb_ref[...]
    act = ln * _sigmoid(ln)
    conv = jnp.dot(act.astype(BF16), wpw_ref[...], preferred_element_type=F32)

    merged = _sigmoid(proj(OFF_GR, D_MODEL)) * ret + _sigmoid(proj(OFF_GC, D_MODEL)) * conv
    mix = jnp.dot(merged.astype(BF16), wout_ref[...], preferred_element_type=F32)
    o_ref[0] = x + g_ref[0] * mix


def _mixer(x, posf, sh1, sc1, g1, ngain, w_in_b, ret_gn_gain, w_dw, b_dw, ln_g, ln_b,
           w_pw2_b, w_out_b):
    B, S, D = x.shape
    T = MIX_TILE
    half = RET_DK // 2
    inv_freq = ROPE_BASE ** (-jnp.arange(half, dtype=F32) / half)
    invf = jnp.concatenate([inv_freq, inv_freq]).reshape(1, RET_DK)
    sgn = jnp.concatenate([-jnp.ones((half,), F32), jnp.ones((half,), F32)]).reshape(1, RET_DK)
    C = RET_CHUNK
    log_gamma = jnp.log1p(-jnp.exp2(-5.0 - jnp.arange(RET_HEADS, dtype=F32)))
    idx = jnp.arange(C, dtype=F32)
    diff = idx[:, None] - idx[None, :]
    dmask = jnp.where(diff[None] >= 0,
                      jnp.exp(jnp.maximum(diff, 0.0)[None] * log_gamma[:, None, None]), 0.0)
    zeta = jnp.exp((C - 1 - idx)[None, :] * log_gamma[:, None])
    xi = jnp.exp((idx + 1)[None, :] * log_gamma[:, None])
    xi_b = jnp.broadcast_to(xi[:, :, None], (RET_HEADS, C, RET_DK))
    zeta_b = jnp.broadcast_to(zeta[:, :, None], (RET_HEADS, C, RET_DK))
    chunk_decay = tuple(float((1.0 - 2.0 ** (-5 - hd)) ** C) for hd in range(RET_HEADS))

    row = lambda b, j: (b, 0, 0)
    tile = lambda b, j: (b, j, 0)
    in_specs = [
        pl.BlockSpec((1, T, D), tile),
        pl.BlockSpec((1, T, 1), tile),
        pl.BlockSpec((1, 1, D), row),
        pl.BlockSpec((1, 1, D), row),
        pl.BlockSpec((1, 1, D), row),
        _const_spec((1, D)),
        _const_spec((D, N_IN)),
        _const_spec((1, RET_DK)),
        _const_spec((1, RET_DK)),
        _const_spec((RET_HEADS, C, C)),
        _const_spec((RET_HEADS, C, RET_DK)),
        _const_spec((RET_HEADS, C, RET_DK)),
        _const_spec((1, D)),
        _const_spec((CONV_KERNEL, D)),
        _const_spec((1, D)),
        _const_spec((1, D)),
        _const_spec((1, D)),
        _const_spec((D, D)),
        _const_spec((D, D)),
    ]
    return pl.pallas_call(
        functools.partial(_mixer_kernel, chunk_decay=chunk_decay),
        grid=(B, S // T),
        in_specs=in_specs,
        out_specs=pl.BlockSpec((1, T, D), tile),
        out_shape=jax.ShapeDtypeStruct((B, S, D), F32),
        scratch_shapes=[pltpu.VMEM((RET_HEADS, RET_DK, RET_DV), F32),
                        pltpu.VMEM((CONV_HALO + T, D), F32)],
        compiler_params=pltpu.CompilerParams(
            dimension_semantics=("arbitrary", "arbitrary"), vmem_limit_bytes=VMEM_LIMIT),
        name="mixer",
    )(x, posf, sh1, sc1, g1, ngain, w_in_b, invf, sgn, dmask, xi_b, zeta_b, ret_gn_gain,
      w_dw, b_dw, ln_g, ln_b, w_pw2_b, w_out_b)


def _top_values(v, n):
    out = []
    work = v
    for r in range(n):
        m = jnp.max(work, axis=0, keepdims=True)
        out.append(m)
        if r + 1 < n:
            work = jnp.where(work == m, -jnp.inf, work)
    return out


def _peer_kernel(x_ref, sh_ref, sc_ref, g_ref, ngain_ref, wqT_ref, k1_ref, k2_ref, u_ref, vT_ref,
                 fgain_ref, o_ref, hT_ref, s2_ref, e2_ref, thr_ref, cf_ref, acc_ref):
    c = pl.program_id(1)
    TB = PEER_TB
    half = PEER_DKEY // 2
    n_sel = PEER_TOPK + 1

    @pl.when(c == 0)
    def _():
        x = x_ref[...]
        ms = jnp.mean(x * x, axis=-1, keepdims=True)
        h = x * lax.rsqrt(ms + EPS) * ngain_ref[...]
        h = h * (1.0 + sc_ref[0]) + sh_ref[0]
        hT = h.T.astype(BF16)
        hT_ref[...] = hT
        qT = jnp.dot(wqT_ref[...], hT, preferred_element_type=F32)
        for hd in range(PEER_HEADS):
            q1 = qT[hd * PEER_DKEY:hd * PEER_DKEY + half].astype(BF16)
            q2 = qT[hd * PEER_DKEY + half:(hd + 1) * PEER_DKEY].astype(BF16)
            s1 = jnp.dot(k1_ref[...], q1, preferred_element_type=F32)
            s2 = jnp.dot(k2_ref[...], q2, preferred_element_type=F32)
            v1 = _top_values(s1, n_sel)
            v2 = _top_values(s2, n_sel)
            cands = [v1[a] + v2[b] for a in range(n_sel) for b in range(n_sel)
                     if (a + 1) * (b + 1) <= n_sel]
            pad = (-len(cands)) % 8
            cands += [jnp.full((1, TB), -jnp.inf, F32)] * pad
            t = _top_values(jnp.concatenate(cands, axis=0), n_sel)
            tau = 0.5 * (t[PEER_TOPK - 1] + t[PEER_TOPK])
            z = jnp.ones((1, TB), F32)
            for r in range(1, PEER_TOPK):
                z = z + jnp.exp(t[r] - t[0])
            s2_ref[hd] = s2
            e2_ref[hd] = jnp.exp(s2 - v2[0])
            thr_ref[hd] = tau - s1
            cf_ref[hd] = jnp.exp(s1 - v1[0]) / z
        acc_ref[...] = jnp.zeros_like(acc_ref)

    aT = jnp.dot(u_ref[...], hT_ref[...], preferred_element_type=F32)
    gel = aT * (0.5 * (1.0 + jnp.tanh(GELU_C0 * (aT + GELU_C1 * (aT * aT * aT)))))
    groups = PEER_EC // PEER_NKEYS
    g_parts = []
    for ii in range(groups):
        i = c * groups + ii
        w = jnp.zeros((PEER_NKEYS, TB), F32)
        for hd in range(PEER_HEADS):
            thr = thr_ref[hd, pl.ds(i, 1), :]
            cf = cf_ref[hd, pl.ds(i, 1), :]
            w = w + jnp.where(s2_ref[hd] >= thr, e2_ref[hd] * cf, 0.0)
        g_parts.append((w * gel[ii * PEER_NKEYS:(ii + 1) * PEER_NKEYS]).astype(BF16))
    gT = jnp.concatenate(g_parts, axis=0)
    acc_ref[...] += jnp.dot(vT_ref[...], gT, preferred_element_type=F32)

    @pl.when(c == pl.num_programs(1) - 1)
    def _():
        x2 = x_ref[...] + g_ref[0] * acc_ref[...].T
        ms = jnp.mean(x2 * x2, axis=-1, keepdims=True)
        o_ref[...] = x2 * lax.rsqrt(ms + EPS) * fgain_ref[...]


def _peer(x1, sh2, sc2, g2, ngain, wqT_b, k1_b, k2_b, u_b, vT_b, fgain, seq):
    Ttok, D = x1.shape
    TB, EC = PEER_TB, PEER_EC
    n_exp = u_b.shape[0]
    per_seq = seq // TB
    tok = lambda t, c: (t, 0)
    row = lambda t, c: (t // per_seq, 0, 0)
    in_specs = [
        pl.BlockSpec((TB, D), tok),
        pl.BlockSpec((1, 1, D), row),
        pl.BlockSpec((1, 1, D), row),
        pl.BlockSpec((1, 1, D), row),
        _const_spec((1, D)),
        _const_spec((PEER_HEADS * PEER_DKEY, D)),
        _const_spec((PEER_NKEYS, PEER_DKEY // 2)),
        _const_spec((PEER_NKEYS, PEER_DKEY // 2)),
        pl.BlockSpec((EC, D), lambda t, c: (c, 0)),
        pl.BlockSpec((D, EC), lambda t, c: (0, c)),
        _const_spec((1, D)),
    ]
    per_head = pltpu.VMEM((PEER_HEADS, PEER_NKEYS, TB), F32)
    return pl.pallas_call(
        _peer_kernel,
        grid=(Ttok // TB, n_exp // EC),
        in_specs=in_specs,
        out_specs=pl.BlockSpec((TB, D), tok),
        out_shape=jax.ShapeDtypeStruct((Ttok, D), F32),
        scratch_shapes=[pltpu.VMEM((D, TB), BF16), per_head, per_head, per_head, per_head,
                        pltpu.VMEM((D, TB), F32)],
        compiler_params=pltpu.CompilerParams(
            dimension_semantics=("arbitrary", "arbitrary"), vmem_limit_bytes=VMEM_LIMIT),
        name="peer",
    )(x1, sh2, sc2, g2, ngain, wqT_b, k1_b, k2_b, u_b, vT_b, fgain)


def kernel(x, c, positions, w_ada, b_ada, norm_mix_gain, w_in, ret_gn_gain, w_dw, b_dw,
           conv_ln_gain, conv_ln_bias, w_pw2, w_out, norm_ffn_gain, w_peer_q, peer_subkeys1,
           peer_subkeys2, peer_u, peer_v, final_norm_gain):
    B, S, D = x.shape
    assert w_ada.shape[0] == 1, "the peer call applies the final norm: single-layer trunk only"
    posf = positions.astype(F32).reshape(B, S, 1)
    r1 = lambda a: a.reshape(1, -1)
    ada = _ada(c, w_ada[0], b_ada[0])
    sh1, sc1, g1, sh2, sc2, g2 = [a.reshape(B, 1, D) for a in jnp.split(ada, N_ADA, axis=-1)]
    x1 = _mixer(x, posf, sh1, sc1, g1, r1(norm_mix_gain[0]), w_in[0].astype(BF16),
                r1(ret_gn_gain[0]), w_dw[0].reshape(CONV_KERNEL, D), r1(b_dw[0]),
                r1(conv_ln_gain[0]), r1(conv_ln_bias[0]), w_pw2[0].astype(BF16),
                w_out[0].astype(BF16))
    out = _peer(x1.reshape(B * S, D), sh2, sc2, g2, r1(norm_ffn_gain[0]),
                w_peer_q[0].T.astype(BF16), peer_subkeys1[0].astype(BF16),
                peer_subkeys2[0].astype(BF16), peer_u[0].astype(BF16),
                peer_v[0].T.astype(BF16), r1(final_norm_gain), S)
    return out.reshape(B, S, D)
```

```python
import functools
import math

import jax
import jax.numpy as jnp
import numpy as np
from jax import lax
from jax.experimental import pallas as pl
from jax.experimental.pallas import tpu as pltpu

F32 = jnp.float32
BF16 = jnp.bfloat16

D_MODEL = 1024
RET_HEADS = 4
RET_DK = 128
RET_DV = 256
RET_QK = RET_HEADS * RET_DK
RET_V = RET_HEADS * RET_DV
RET_CHUNK = 128
ROPE_BASE = 10000.0
CONV_KERNEL = 31
CONV_HALO = 32
PEER_HEADS = 8
PEER_DKEY = 256
PEER_NKEYS = 128
PEER_TOPK = 16
EPS = 1e-6
N_ADA = 6

OFF_Q = 0
OFF_K = OFF_Q + RET_QK
OFF_V = OFF_K + RET_QK
OFF_G = OFF_V + RET_V
OFF_CV = OFF_G + RET_V
OFF_CG = OFF_CV + D_MODEL
OFF_GR = OFF_CG + D_MODEL
OFF_GC = OFF_GR + D_MODEL
N_IN = OFF_GC + D_MODEL

MIX_TILE = 256
PEER_TB = 512
PEER_EC = 512
VMEM_LIMIT = 56 * 1024 * 1024
LANES = 128
MXU_N = 256
BF16_ROWS = 16
RANK_FAR = 127.0

GELU_C0 = math.sqrt(2.0 / math.pi)
GELU_C1 = 0.044715


def _sigmoid(v):
    return 1.0 / (1.0 + jnp.exp(-v))


def _const_spec(shape):
    nd = len(shape)
    return pl.BlockSpec(shape, lambda *_: (0,) * nd, pipeline_mode=pl.Buffered(1))


def _ada_kernel(c_ref, w_ref, b_ref, o_ref):
    cv = c_ref[...]
    cond = cv * _sigmoid(cv)
    o_ref[...] = jnp.dot(cond, w_ref[...], preferred_element_type=F32,
                         precision=lax.Precision.HIGHEST) + b_ref[...]


def _ada(c, w_ada, b_ada):
    B, D = c.shape
    N = w_ada.shape[1]
    tn = 1024
    return pl.pallas_call(
        _ada_kernel,
        grid=(N // tn,),
        in_specs=[pl.BlockSpec((B, D), lambda j: (0, 0)),
                  pl.BlockSpec((D, tn), lambda j: (0, j)),
                  pl.BlockSpec((1, tn), lambda j: (0, j))],
        out_specs=pl.BlockSpec((B, tn), lambda j: (0, j)),
        out_shape=jax.ShapeDtypeStruct((B, N), F32),
        compiler_params=pltpu.CompilerParams(vmem_limit_bytes=VMEM_LIMIT),
        name="ada",
    )(c, w_ada, b_ada.reshape(1, N))


def _mixer_kernel(x_ref, pos_ref, sh_ref, sc_ref, g_ref, ngain_ref, win_ref, invf_ref, sgn_ref,
                  dmask_ref, xi_ref, zeta_ref, gng_ref, wdw_ref, bdw_ref, lng_ref, lnb_ref,
                  wpw_ref, wout_ref, o_ref, state_ref, ubuf_ref, *, chunk_decay):
    T = MIX_TILE

    @pl.when(pl.program_id(1) == 0)
    def _():
        state_ref[...] = jnp.zeros_like(state_ref)
        ubuf_ref[0:CONV_HALO, :] = jnp.zeros((CONV_HALO, D_MODEL), F32)

    x = x_ref[0]
    ms = jnp.mean(x * x, axis=-1, keepdims=True)
    h = x * lax.rsqrt(ms + EPS) * ngain_ref[...]
    h = h * (1.0 + sc_ref[0]) + sh_ref[0]
    hb = h.astype(BF16)

    def proj(off, width):
        return jnp.dot(hb, win_ref[:, off:off + width], preferred_element_type=F32)

    ang = pos_ref[0] * invf_ref[...]
    cos = jnp.cos(ang)
    sin_signed = jnp.sin(ang) * sgn_ref[...]

    def rot(t):
        return t * cos + pltpu.roll(t, RET_DK // 2, 1) * sin_signed

    ret_heads = []
    for hd in range(RET_HEADS):
        qh = rot(proj(OFF_Q + hd * RET_DK, RET_DK))
        kh = rot(proj(OFF_K + hd * RET_DK, RET_DK)) * (RET_DK ** -0.5)
        vh = proj(OFF_V + hd * RET_DV, RET_DV)
        st = state_ref[hd]
        outs = []
        for ci in range(T // RET_CHUNK):
            rows = slice(ci * RET_CHUNK, (ci + 1) * RET_CHUNK)
            qc, kc, vc = qh[rows], kh[rows], vh[rows]
            qcb, kcb, vcb = qc.astype(BF16), kc.astype(BF16), vc.astype(BF16)
            scores = lax.dot_general(qcb, kcb, (((1,), (1,)), ((), ())),
                                     preferred_element_type=F32) * dmask_ref[hd]
            inner = jnp.dot(scores.astype(BF16), vcb, preferred_element_type=F32)
            cross = jnp.dot((qc * xi_ref[hd]).astype(BF16), st.astype(BF16),
                            preferred_element_type=F32)
            kz = (kc * zeta_ref[hd]).astype(BF16)
            st = st * chunk_decay[hd] + lax.dot_general(
                kz, vcb, (((0,), (0,)), ((), ())), preferred_element_type=F32)
            outs.append(inner + cross)
        state_ref[hd] = st
        y = jnp.concatenate(outs, axis=0)
        mu = jnp.mean(y, axis=-1, keepdims=True)
        yc = y - mu
        var = jnp.mean(yc * yc, axis=-1, keepdims=True)
        ret_heads.append(yc * lax.rsqrt(var + EPS))
    ret = jnp.concatenate(ret_heads, axis=1) * gng_ref[...]
    g_ret = proj(OFF_G, RET_V)
    ret = g_ret * _sigmoid(g_ret) * ret

    u = proj(OFF_CV, D_MODEL) * _sigmoid(proj(OFF_CG, D_MODEL))
    ubuf_ref[CONV_HALO:CONV_HALO + T, :] = u
    acc = jnp.broadcast_to(bdw_ref[...], (T, D_MODEL))
    base = CONV_HALO - (CONV_KERNEL - 1)
    for k in range(CONV_KERNEL):
        acc = acc + ubuf_ref[base + k:base + k + T, :] * wdw_ref[k:k + 1, :]
    ubuf_ref[0:CONV_HALO, :] = ubuf_ref[T:T + CONV_HALO, :]
    mu = jnp.mean(acc, axis=-1, keepdims=True)
    ac = acc - mu
    var = jnp.mean(ac * ac, axis=-1, keepdims=True)
    ln = ac * lax.rsqrt(var + EPS) * lng_ref[...] + lnb_ref[...]
    act = ln * _sigmoid(ln)
    conv = jnp.dot(act.astype(BF16), wpw_ref[...], preferred_element_type=F32)

    merged = _sigmoid(proj(OFF_GR, D_MODEL)) * ret + _sigmoid(proj(OFF_GC, D_MODEL)) * conv
    mix = jnp.dot(merged.astype(BF16), wout_ref[...], preferred_element_type=F32)
    o_ref[0] = x + g_ref[0] * mix


def _mixer(x, posf, sh1, sc1, g1, ngain, w_in_b, ret_gn_gain, w_dw, b_dw, ln_g, ln_b,
           w_pw2_b, w_out_b):
    B, S, D = x.shape
    T = MIX_TILE
    half = RET_DK // 2
    inv_freq = ROPE_BASE ** (-jnp.arange(half, dtype=F32) / half)
    invf = jnp.concatenate([inv_freq, inv_freq]).reshape(1, RET_DK)
    sgn = jnp.concatenate([-jnp.ones((half,), F32), jnp.ones((half,), F32)]).reshape(1, RET_DK)
    C = RET_CHUNK
    log_gamma = jnp.log1p(-jnp.exp2(-5.0 - jnp.arange(RET_HEADS, dtype=F32)))
    idx = jnp.arange(C, dtype=F32)
    diff = idx[:, None] - idx[None, :]
    dmask = jnp.where(diff[None] >= 0,
                      jnp.exp(jnp.maximum(diff, 0.0)[None] * log_gamma[:, None, None]), 0.0)
    zeta = jnp.exp((C - 1 - idx)[None, :] * log_gamma[:, None])
    xi = jnp.exp((idx + 1)[None, :] * log_gamma[:, None])
    xi_b = jnp.broadcast_to(xi[:, :, None], (RET_HEADS, C, RET_DK))
    zeta_b = jnp.broadcast_to(zeta[:, :, None], (RET_HEADS, C, RET_DK))
    chunk_decay = tuple(float((1.0 - 2.0 ** (-5 - hd)) ** C) for hd in range(RET_HEADS))

    row = lambda b, j: (b, 0, 0)
    tile = lambda b, j: (b, j, 0)
    in_specs = [
        pl.BlockSpec((1, T, D), tile),
        pl.BlockSpec((1, T, 1), tile),
        pl.BlockSpec((1, 1, D), row),
        pl.BlockSpec((1, 1, D), row),
        pl.BlockSpec((1, 1, D), row),
        _const_spec((1, D)),
        _const_spec((D, N_IN)),
        _const_spec((1, RET_DK)),
        _const_spec((1, RET_DK)),
        _const_spec((RET_HEADS, C, C)),
        _const_spec((RET_HEADS, C, RET_DK)),
        _const_spec((RET_HEADS, C, RET_DK)),
        _const_spec((1, D)),
        _const_spec((CONV_KERNEL, D)),
        _const_spec((1, D)),
        _const_spec((1, D)),
        _const_spec((1, D)),
        _const_spec((D, D)),
        _const_spec((D, D)),
    ]
    return pl.pallas_call(
        functools.partial(_mixer_kernel, chunk_decay=chunk_decay),
        grid=(B, S // T),
        in_specs=in_specs,
        out_specs=pl.BlockSpec((1, T, D), tile),
        out_shape=jax.ShapeDtypeStruct((B, S, D), F32),
        scratch_shapes=[pltpu.VMEM((RET_HEADS, RET_DK, RET_DV), F32),
                        pltpu.VMEM((CONV_HALO + T, D), F32)],
        compiler_params=pltpu.CompilerParams(
            dimension_semantics=("arbitrary", "arbitrary"), vmem_limit_bytes=VMEM_LIMIT),
        name="mixer",
    )(x, posf, sh1, sc1, g1, ngain, w_in_b, invf, sgn, dmask, xi_b, zeta_b, ret_gn_gain,
      w_dw, b_dw, ln_g, ln_b, w_pw2_b, w_out_b)


def _top_values(v, n, n_ranked=0):
    out = []
    work = v
    rank = jnp.full(v.shape, RANK_FAR, F32) if n_ranked else None
    for r in range(n):
        m = jnp.max(work, axis=0, keepdims=True)
        out.append(m)
        hit = work == m
        if r < n_ranked:
            rank = jnp.where(hit, float(r), rank)
        if r + 1 < n:
            work = jnp.where(hit, -jnp.inf, work)
    return out, rank


def _peer_prologue(x_ref, sh_ref, sc_ref, ngain_ref, wqT_ref, k1_ref, k2_ref,
                   hT_ref, r2_ref, e2_ref, n_ref, cf_ref):
    TB = PEER_TB
    half = PEER_DKEY // 2
    n_sel = PEER_TOPK + 1
    n_pad = 24
    x = x_ref[...]
    ms = jnp.mean(x * x, axis=-1, keepdims=True)
    h = x * lax.rsqrt(ms + EPS) * ngain_ref[...]
    h = h * (1.0 + sc_ref[0]) + sh_ref[0]
    hT = h.T.astype(BF16)
    hT_ref[...] = hT
    qT = jnp.dot(wqT_ref[...], hT, preferred_element_type=F32)
    neg = jnp.full((n_pad - n_sel, LANES), -jnp.inf, F32)
    for hd in range(PEER_HEADS):
        q1 = qT[hd * PEER_DKEY:hd * PEER_DKEY + half].astype(BF16)
        q2 = qT[hd * PEER_DKEY + half:(hd + 1) * PEER_DKEY].astype(BF16)
        s1 = jnp.dot(k1_ref[...], q1, preferred_element_type=F32)
        s2 = jnp.dot(k2_ref[...], q2, preferred_element_type=F32)
        for lg in range(TB // LANES):
            lanes = slice(lg * LANES, (lg + 1) * LANES)
            s1t, s2t = s1[:, lanes], s2[:, lanes]
            v1, _ = _top_values(s1t, n_sel)
            v2, rank2 = _top_values(s2t, n_sel, PEER_TOPK)
            v1p = jnp.concatenate(v1 + [neg], axis=0)
            v2p = jnp.concatenate(v2 + [neg], axis=0)
            cands = [v1p + v2[0], v2p + v1[0]] + [v2p[0:8] + v1[a] for a in range(1, 8)]
            t, _ = _top_values(jnp.concatenate(cands, axis=0), n_sel)
            tau = 0.5 * (t[PEER_TOPK - 1] + t[PEER_TOPK])
            z = jnp.ones((1, LANES), F32)
            for r in range(1, PEER_TOPK):
                z = z + jnp.exp(t[r] - t[0])
            thr = tau - s1t
            cnt = jnp.zeros((PEER_NKEYS, LANES), F32)
            for b in range(PEER_TOPK):
                cnt = jnp.where(v2[b] >= thr, float(b + 1), cnt)
            r2_ref[hd, :, lanes] = rank2.astype(BF16)
            e2_ref[hd, :, lanes] = jnp.exp(s2t - v2[0]).astype(BF16)
            n_ref[hd, :, lanes] = cnt
            cf_ref[hd, :, lanes] = jnp.exp(s1t - v1[0]) / z


def _peer_kernel(x_ref, sh_ref, sc_ref, g_ref, ngain_ref, wqT_ref, k1_ref, k2_ref, u_ref, vT_ref,
                 fgain_ref, o_ref, hT_ref, r2_ref, e2_ref, n_ref, cf_ref, a_ref, gt_ref,
                 acc_ref):
    c = pl.program_id(1)
    n_chunks = pl.num_programs(1) - 2
    TB = PEER_TB

    @pl.when(c == 0)
    def _():
        _peer_prologue(x_ref, sh_ref, sc_ref, ngain_ref, wqT_ref, k1_ref, k2_ref,
                       hT_ref, r2_ref, e2_ref, n_ref, cf_ref)
        a_ref[...] = jnp.zeros_like(a_ref)
        gt_ref[...] = jnp.zeros_like(gt_ref)
        acc_ref[...] = jnp.zeros_like(acc_ref)

    cur = c % 2
    groups = PEER_EC // PEER_NKEYS
    first = jnp.clip(c - 1, 0, n_chunks - 1) * groups
    n_rows = [[n_ref[hd, pl.ds(first + ii, 1), :] for hd in range(PEER_HEADS)]
              for ii in range(groups)]
    cf_rows = [[cf_ref[hd, pl.ds(first + ii, 1), :] for hd in range(PEER_HEADS)]
               for ii in range(groups)]

    def u_piece(nn):
        cols = slice(nn * MXU_N, (nn + 1) * MXU_N)
        a_ref[cur, :, cols] = jnp.dot(u_ref[...], hT_ref[:, cols], preferred_element_type=F32)

    def v_piece(kk, nn):
        ks = slice(kk * MXU_N, (kk + 1) * MXU_N)
        cols = slice(nn * MXU_N, (nn + 1) * MXU_N)
        acc_ref[:, cols] += jnp.dot(vT_ref[:, ks], gt_ref[cur, ks, cols],
                                    preferred_element_type=F32)

    def gate(lg):
        lanes = slice(lg * LANES, (lg + 1) * LANES)
        n_sub = PEER_NKEYS // BF16_ROWS
        w = [[jnp.zeros((BF16_ROWS, LANES), BF16) for _ in range(n_sub)] for _ in range(groups)]
        for hd in range(PEER_HEADS):
            nb = [jnp.broadcast_to(n_rows[ii][hd][:, lanes], (BF16_ROWS, LANES)).astype(BF16)
                  for ii in range(groups)]
            cb = [jnp.broadcast_to(cf_rows[ii][hd][:, lanes], (BF16_ROWS, LANES)).astype(BF16)
                  for ii in range(groups)]
            for jb in range(n_sub):
                sub = slice(jb * BF16_ROWS, (jb + 1) * BF16_ROWS)
                rank = r2_ref[hd, sub, lanes]
                e2 = e2_ref[hd, sub, lanes]
                for ii in range(groups):
                    w[ii][jb] = w[ii][jb] + jnp.where(rank < nb[ii], e2 * cb[ii],
                                                      jnp.zeros_like(e2))
        for ii in range(groups):
            for jb in range(n_sub):
                rows = slice(ii * PEER_NKEYS + jb * BF16_ROWS,
                             ii * PEER_NKEYS + (jb + 1) * BF16_ROWS)
                a = a_ref[1 - cur, rows, lanes]
                gel = a * (0.5 * (1.0 + jnp.tanh(GELU_C0 * (a + GELU_C1 * (a * a * a)))))
                gt_ref[1 - cur, rows, lanes] = w[ii][jb] * gel.astype(BF16)

    mxu_work = [functools.partial(u_piece, nn) for nn in range(TB // MXU_N)]
    mxu_work += [functools.partial(v_piece, kk, nn)
                 for nn in range(TB // MXU_N) for kk in range(PEER_EC // MXU_N)]
    per = -(-len(mxu_work) // (TB // LANES))
    for lg in range(TB // LANES):
        for m in mxu_work[lg * per:(lg + 1) * per]:
            m()
        gate(lg)

    @pl.when(c == n_chunks + 1)
    def _():
        x2 = x_ref[...] + g_ref[0] * acc_ref[...].T
        ms = jnp.mean(x2 * x2, axis=-1, keepdims=True)
        o_ref[...] = x2 * lax.rsqrt(ms + EPS) * fgain_ref[...]


def _peer(x1, sh2, sc2, g2, ngain, wqT_b, k1_b, k2_b, u_b, vT_b, fgain, seq):
    Ttok, D = x1.shape
    TB, EC = PEER_TB, PEER_EC
    n_chunks = u_b.shape[0] // EC
    per_seq = seq // TB
    tok = lambda t, c: (t, 0)
    row = lambda t, c: (t // per_seq, 0, 0)
    in_specs = [
        pl.BlockSpec((TB, D), tok),
        pl.BlockSpec((1, 1, D), row),
        pl.BlockSpec((1, 1, D), row),
        pl.BlockSpec((1, 1, D), row),
        _const_spec((1, D)),
        _const_spec((PEER_HEADS * PEER_DKEY, D)),
        _const_spec((PEER_NKEYS, PEER_DKEY // 2)),
        _const_spec((PEER_NKEYS, PEER_DKEY // 2)),
        pl.BlockSpec((EC, D), lambda t, c: (jnp.minimum(c, n_chunks - 1), 0)),
        pl.BlockSpec((D, EC), lambda t, c: (0, jnp.clip(c - 2, 0, n_chunks - 1))),
        _const_spec((1, D)),
    ]
    per_head = pltpu.VMEM((PEER_HEADS, PEER_NKEYS, TB), F32)
    per_head_b = pltpu.VMEM((PEER_HEADS, PEER_NKEYS, TB), BF16)
    return pl.pallas_call(
        _peer_kernel,
        grid=(Ttok // TB, n_chunks + 2),
        in_specs=in_specs,
        out_specs=pl.BlockSpec((TB, D), tok),
        out_shape=jax.ShapeDtypeStruct((Ttok, D), F32),
        scratch_shapes=[pltpu.VMEM((D, TB), BF16), per_head_b, per_head_b, per_head, per_head,
                        pltpu.VMEM((2, EC, TB), F32), pltpu.VMEM((2, EC, TB), BF16),
                        pltpu.VMEM((D, TB), F32)],
        compiler_params=pltpu.CompilerParams(
            dimension_semantics=("arbitrary", "arbitrary"), vmem_limit_bytes=VMEM_LIMIT),
        name="peer",
    )(x1, sh2, sc2, g2, ngain, wqT_b, k1_b, k2_b, u_b, vT_b, fgain)


def kernel(x, c, positions, w_ada, b_ada, norm_mix_gain, w_in, ret_gn_gain, w_dw, b_dw,
           conv_ln_gain, conv_ln_bias, w_pw2, w_out, norm_ffn_gain, w_peer_q, peer_subkeys1,
           peer_subkeys2, peer_u, peer_v, final_norm_gain):
    B, S, D = x.shape
    assert w_ada.shape[0] == 1, "the peer call applies the final norm: single-layer trunk only"
    posf = positions.astype(F32).reshape(B, S, 1)
    r1 = lambda a: a.reshape(1, -1)
    ada = _ada(c, w_ada[0], b_ada[0])
    sh1, sc1, g1, sh2, sc2, g2 = [a.reshape(B, 1, D) for a in jnp.split(ada, N_ADA, axis=-1)]
    x1 = _mixer(x, posf, sh1, sc1, g1, r1(norm_mix_gain[0]), w_in[0].astype(BF16),
                r1(ret_gn_gain[0]), w_dw[0].reshape(CONV_KERNEL, D), r1(b_dw[0]),
                r1(conv_ln_gain[0]), r1(conv_ln_bias[0]), w_pw2[0].astype(BF16),
                w_out[0].astype(BF16))
    out = _peer(x1.reshape(B * S, D), sh2, sc2, g2, r1(norm_ffn_gain[0]),
                w_peer_q[0].T.astype(BF16), peer_subkeys1[0].astype(BF16),
                peer_subkeys2[0].astype(BF16), peer_u[0].astype(BF16),
                peer_v[0].T.astype(BF16), r1(final_norm_gain), S)
    return out.reshape(B, S, D)
```
